```python
import math
import jax, jax.numpy as jnp
from jax import lax
import numpy as np

D_MODEL = 4096
BATCH = 1
SEQ = 8192
DEPTH = 1

MEM_LEN = 256
DA_HEADS = 8
DA_HEAD_DIM = 128
DA_V_DIM = 2 * DA_HEAD_DIM
DA_WIDTH = DA_HEADS * DA_V_DIM
HG_HEADS = 16
HG_EXPAND = 128
HG_HEAD_DIM = 128
HG_WIDTH = HG_HEADS * HG_HEAD_DIM
HG_CHUNK = 64
XA_HEADS = 4
XA_HEAD_DIM = 128
XA_WIDTH = XA_HEADS * XA_HEAD_DIM
D_FF = 11008
ROPE_THETA = 10000.0
Q_BLOCK = 128
LN_EPS = 1e-5
DEEPNORM_ALPHA = (2 * DEPTH) ** 0.25
DEEPNORM_BETA = (8 * DEPTH) ** -0.25
MIX_SPLITS = (DA_HEADS * 2 * DA_HEAD_DIM,
              DA_HEADS * 2 * DA_HEAD_DIM,
              DA_WIDTH,
              HG_HEADS * HG_EXPAND,
              HG_HEADS * HG_EXPAND,
              HG_WIDTH,
              HG_WIDTH,
              D_MODEL,
              D_MODEL)
MIX_IN_WIDTH = sum(MIX_SPLITS)

kernel_name = "diffattn_hgrn2_gated_hybrid_layer"


def layer_norm(x, g, b):
    xf = x.astype(jnp.float32)
    mu = jnp.mean(xf, axis=-1, keepdims=True)
    var = jnp.mean(jnp.square(xf - mu), axis=-1, keepdims=True)
    y = (xf - mu) * lax.rsqrt(var + LN_EPS) * g.astype(jnp.float32) + b.astype(jnp.float32)
    return y.astype(x.dtype)


def rms_norm(x, g):
    xf = x.astype(jnp.float32)
    y = xf * lax.rsqrt(jnp.mean(xf * xf, axis=-1, keepdims=True) + LN_EPS) * g.astype(jnp.float32)
    return y.astype(x.dtype)


def swiglu(x, w_gate, w_up, w_down):
    return (jax.nn.silu(x @ w_gate) * (x @ w_up)) @ w_down


def rope_tables(seq, dim):
    inv = 1.0 / (ROPE_THETA ** (jnp.arange(0, dim, 2, dtype=jnp.float32) / dim))
    ang = jnp.arange(seq, dtype=jnp.float32)[:, None] * inv[None, :]
    return jnp.cos(ang), jnp.sin(ang)


def apply_rope(x, cos, sin):
    x1, x2 = jnp.split(x.astype(jnp.float32), 2, axis=-1)
    c = cos[None, :, None, :]
    s = sin[None, :, None, :]
    return jnp.concatenate([x1 * c - x2 * s, x2 * c + x1 * s], axis=-1).astype(x.dtype)


def causal_diff_attention(q, k, v, lam):
    B, S, H, _, Dh = q.shape
    nb = S // Q_BLOCK
    scale = Dh ** -0.5
    qb = q.reshape(B, nb, Q_BLOCK, H, 2, Dh).transpose(1, 0, 2, 3, 4, 5)
    kpos = jnp.arange(S)

    def block(args):
        qblk, start = args
        s = jnp.einsum('bqhmd,bkhmd->bmhqk', qblk, k).astype(jnp.float32) * scale
        qpos = start + jnp.arange(Q_BLOCK)
        mask = kpos[None, :] <= qpos[:, None]
        p = jax.nn.softmax(jnp.where(mask, s, -jnp.inf), axis=-1)
        a = p[:, 0] - lam * p[:, 1]
        return jnp.einsum('bhqk,bkhe->bqhe', a.astype(v.dtype), v)

    out = lax.map(block, (qb, jnp.arange(nb) * Q_BLOCK))
    return out.transpose(1, 0, 2, 3, 4).reshape(B, S, H, v.shape[-1])


def diff_attention_branch(q_flat, k_flat, v_flat, lq1, lk1, lq2, lk2, subln_g, w_up, cos, sin, layer_idx):
    B, S, _ = q_flat.shape
    lambda_init = 0.8 - 0.6 * math.exp(-0.3 * layer_idx)
    q = apply_rope(q_flat.reshape(B, S, DA_HEADS * 2, DA_HEAD_DIM), cos, sin).reshape(B, S, DA_HEADS, 2, DA_HEAD_DIM)
    k = apply_rope(k_flat.reshape(B, S, DA_HEADS * 2, DA_HEAD_DIM), cos, sin).reshape(B, S, DA_HEADS, 2, DA_HEAD_DIM)
    v = v_flat.reshape(B, S, DA_HEADS, DA_V_DIM)
    f32 = jnp.float32
    lam = (jnp.exp(jnp.sum(lq1.astype(f32) * lk1.astype(f32)))
           - jnp.exp(jnp.sum(lq2.astype(f32) * lk2.astype(f32))) + lambda_init)
    o = causal_diff_attention(q, k, v, lam)
    o = rms_norm(o, subln_g) * (1.0 - lambda_init)
    return o.reshape(B, S, DA_WIDTH) @ w_up


def hgrn2_chunk_scan(q, k, v, logf):
    B, S, H, N = q.shape
    Dv = v.shape[-1]
    C = HG_CHUNK
    nc = S // C

    def to_chunks(t):
        return t.reshape(B, nc, C, H, t.shape[-1]).transpose(1, 0, 3, 2, 4)

    qc, kc, vc, gc = to_chunks(q), to_chunks(k), to_chunks(v), to_chunks(logf)
    bc = jnp.cumsum(gc, axis=3)
    causal = jnp.tril(jnp.ones((C, C), dtype=bool))

    def step(state, inp):
        qt, kt, vt, bt = inp
        o_inter = jnp.einsum('bhcn,bhnv->bhcv', qt * jnp.exp(bt), state)
        diff = bt[:, :, :, None, :] - bt[:, :, None, :, :]
        decay = jnp.exp(jnp.where(causal[None, None, :, :, None], diff, -jnp.inf))
        A = jnp.einsum('bhtn,bhsn,bhtsn->bhts', qt, kt, decay)
        o_intra = jnp.einsum('bhts,bhsv->bhtv', A, vt)
        bl = bt[:, :, -1:, :]
        new_state = (jnp.exp(bl[:, :, 0, :])[..., None] * state
                     + jnp.einsum('bhsn,bhsv->bhnv', kt * jnp.exp(bl - bt), vt))
        return new_state, o_inter + o_intra

    s0 = jnp.zeros((B, H, N, Dv), jnp.float32)
    _, o = lax.scan(step, s0, (qc, kc, vc, bc))
    return o.transpose(1, 0, 3, 2, 4).reshape(B, S, H, Dv)


def hgrn2_branch(q_flat, f_flat, i_flat, g_flat, lb, norm_g, w_up):
    B, S, _ = q_flat.shape
    f32 = jnp.float32
    shp_k = (B, S, HG_HEADS, HG_EXPAND)
    shp_v = (B, S, HG_HEADS, HG_HEAD_DIM)
    lbh = lb.reshape(HG_HEADS, HG_EXPAND)
    f = lbh + (1.0 - lbh) * jax.nn.sigmoid(f_flat.astype(f32).reshape(shp_k))
    o = hgrn2_chunk_scan(q_flat.astype(f32).reshape(shp_k), 1.0 - f,
                         i_flat.astype(f32).reshape(shp_v), jnp.log(f))
    o = rms_norm(o, norm_g) * jax.nn.silu(g_flat.astype(f32).reshape(shp_v))
    return o.astype(q_flat.dtype).reshape(B, S, HG_WIDTH) @ w_up


def memory_cross_attention(x, mem, w_q, w_k, w_v, w_o):
    B, S, _ = x.shape
    M = mem.shape[1]
    q = (x @ w_q).reshape(B, S, XA_HEADS, XA_HEAD_DIM)
    k = (mem @ w_k).reshape(B, M, XA_HEADS, XA_HEAD_DIM)
    v = (mem @ w_v).reshape(B, M, XA_HEADS, XA_HEAD_DIM)
    s = jnp.einsum('bshd,bmhd->bhsm', q, k).astype(jnp.float32) * (XA_HEAD_DIM ** -0.5)
    p = jax.nn.softmax(s, axis=-1).astype(v.dtype)
    o = jnp.einsum('bhsm,bmhd->bshd', p, v).reshape(B, S, XA_WIDTH)
    return o @ w_o


def setup_inputs(seed: int = 0) -> dict:
    key = jax.random.key(seed)
    ks = iter(jax.random.split(key, 40))
    L = DEPTH
    f32 = jnp.float32

    def w(shape, fan_in, scale=1.0):
        return jax.random.normal(next(ks), shape, f32) * (scale * fan_in ** -0.5)

    def gain(shape):
        return 1.0 + 0.02 * jax.random.normal(next(ks), shape, f32)

    def bias(shape):
        return 0.02 * jax.random.normal(next(ks), shape, f32)

    return {
        "x": jax.random.normal(next(ks), (BATCH, SEQ, D_MODEL), f32),
        "mem": jax.random.normal(next(ks), (BATCH, MEM_LEN, D_MODEL), f32),
        "ffn1_w_gate": w((L, D_MODEL, D_FF), D_MODEL),
        "ffn1_w_up": w((L, D_MODEL, D_FF), D_MODEL),
        "ffn1_w_down": w((L, D_FF, D_MODEL), D_FF, DEEPNORM_BETA),
        "ln1_g": gain((L, D_MODEL)),
        "ln1_b": bias((L, D_MODEL)),
        "mix_w_in": w((L, D_MODEL, MIX_IN_WIDTH), D_MODEL),
        "da_lambda_q1": 0.1 * jax.random.normal(next(ks), (L, DA_HEAD_DIM), f32),
        "da_lambda_k1": 0.1 * jax.random.normal(next(ks), (L, DA_HEAD_DIM), f32),
        "da_lambda_q2": 0.1 * jax.random.normal(next(ks), (L, DA_HEAD_DIM), f32),
        "da_lambda_k2": 0.1 * jax.random.normal(next(ks), (L, DA_HEAD_DIM), f32),
        "da_subln_g": gain((L, DA_V_DIM)),
        "da_w_up": w((L, DA_WIDTH, D_MODEL), DA_WIDTH),
        "hg_lb_logits": 0.5 * jax.random.normal(next(ks), (DEPTH + 1, HG_HEADS * HG_EXPAND), f32),
        "hg_norm_g": gain((L, HG_HEAD_DIM)),
        "hg_w_up": w((L, HG_WIDTH, D_MODEL), HG_WIDTH),
        "mix_w_out": w((L, D_MODEL, D_MODEL), D_MODEL, DEEPNORM_BETA),
        "ln2_g": gain((L, D_MODEL)),
        "ln2_b": bias((L, D_MODEL)),
        "xa_w_q": w((L, D_MODEL, XA_WIDTH), D_MODEL),
        "xa_w_k": w((L, D_MODEL, XA_WIDTH), D_MODEL),
        "xa_w_v": w((L, D_MODEL, XA_WIDTH), D_MODEL),
        "xa_w_o": w((L, XA_WIDTH, D_MODEL), XA_WIDTH, DEEPNORM_BETA),
        "ln3_g": gain((L, D_MODEL)),
        "ln3_b": bias((L, D_MODEL)),
        "ffn2_w_gate": w((L, D_MODEL, D_FF), D_MODEL),
        "ffn2_w_up": w((L, D_MODEL, D_FF), D_MODEL),
        "ffn2_w_down": w((L, D_FF, D_MODEL), D_FF, DEEPNORM_BETA),
        "ln4_g": gain((L, D_MODEL)),
        "ln4_b": bias((L, D_MODEL)),
    }


def reference(x, mem, ffn1_w_gate, ffn1_w_up, ffn1_w_down, ln1_g, ln1_b,
              mix_w_in, da_lambda_q1, da_lambda_k1, da_lambda_q2, da_lambda_k2,
              da_subln_g, da_w_up, hg_lb_logits, hg_norm_g, hg_w_up, mix_w_out,
              ln2_g, ln2_b, xa_w_q, xa_w_k, xa_w_v, xa_w_o, ln3_g, ln3_b,
              ffn2_w_gate, ffn2_w_up, ffn2_w_down, ln4_g, ln4_b):
    S = x.shape[1]
    cos, sin = rope_tables(S, DA_HEAD_DIM)
    lb_all = jnp.cumsum(jax.nn.softmax(hg_lb_logits.astype(jnp.float32), axis=0), axis=0)
    split_points = [int(p) for p in np.cumsum(np.array(MIX_SPLITS[:-1]))]
    for l in range(DEPTH):
        x = layer_norm(DEEPNORM_ALPHA * x + 0.5 * swiglu(x, ffn1_w_gate[l], ffn1_w_up[l], ffn1_w_down[l]),
                       ln1_g[l], ln1_b[l])
        proj = x @ mix_w_in[l]
        da_q, da_k, da_v, hg_q, hg_f, hg_i, hg_g, gate_a, gate_b = jnp.split(proj, split_points, axis=-1)
        y_a = diff_attention_branch(da_q, da_k, da_v, da_lambda_q1[l], da_lambda_k1[l],
                                    da_lambda_q2[l], da_lambda_k2[l], da_subln_g[l], da_w_up[l],
                                    cos, sin, l)
        y_b = hgrn2_branch(hg_q, hg_f, hg_i, hg_g, lb_all[l], hg_norm_g[l], hg_w_up[l])
        merged = jax.nn.sigmoid(gate_a) * y_a + jax.nn.sigmoid(gate_b) * y_b
        x = layer_norm(DEEPNORM_ALPHA * x + merged @ mix_w_out[l], ln2_g[l], ln2_b[l])
        x = layer_norm(DEEPNORM_ALPHA * x + memory_cross_attention(x, mem, xa_w_q[l], xa_w_k[l], xa_w_v[l], xa_w_o[l]),
                       ln3_g[l], ln3_b[l])
        x = layer_norm(DEEPNORM_ALPHA * x + 0.5 * swiglu(x, ffn2_w_gate[l], ffn2_w_up[l], ffn2_w_down[l]),
                       ln4_g[l], ln4_b[l])
    return x
```

```python
import functools
import math

import jax
import jax.numpy as jnp
from jax import lax
from jax.experimental import pallas as pl
from jax.experimental.pallas import tpu as pltpu

F32 = jnp.float32
BF16 = jnp.bfloat16

V7X_VMEM_BYTES = 64 * 1024 * 1024
V7X_LANES = 128
V7X_SUBLANES = 8
VMEM_LIMIT_BYTES = V7X_VMEM_BYTES - 8 * 1024 * 1024

LN_EPS = 1e-5
ROPE_THETA = 10000.0
DA_HEAD_DIM = 128
HG_DIM = 128
XA_HEAD_DIM = 128
HG_CHUNK = 128
LN_ROWS = V7X_SUBLANES
RES_LN_COLS = 1024


def _params(n_axes):
    return pltpu.CompilerParams(
        dimension_semantics=("arbitrary",) * n_axes,
        vmem_limit_bytes=VMEM_LIMIT_BYTES,
    )


def _nt_dot(a, b):
    return lax.dot_general(a, b, (((1,), (1,)), ((), ())), preferred_element_type=F32)


def _tn_dot(a, b):
    return lax.dot_general(a, b, (((0,), (0,)), ((), ())), preferred_element_type=F32)


def _ws_kernel(*refs, n_x, x_index, n_extra, epilogue):
    n_w = len(x_index)
    x_refs = refs[:n_x]
    w_refs = refs[n_x:n_x + n_w]
    e_refs = refs[n_x + n_w:n_x + n_w + n_extra]
    o_ref = refs[n_x + n_w + n_extra]
    wb_refs = refs[n_x + n_w + n_extra + 1:]

    @pl.when(pl.program_id(1) == 0)
    def _():
        for w_ref, wb_ref in zip(w_refs, wb_refs):
            wb_ref[...] = w_ref[...].astype(BF16)

    prods = [jnp.dot(x_refs[xi][...], wb_ref[...], preferred_element_type=F32)
             for xi, wb_ref in zip(x_index, wb_refs)]
    o_ref[...] = epilogue(prods, [e[...] for e in e_refs]).astype(o_ref.dtype)


def ws_matmul(name, xs, ws, x_index, col0, ncols, tm, tn, out_dtype, epilogue, extras=()):
    M = xs[0].shape[0]
    assert M % tm == 0 and ncols % tn == 0 and col0 % tn == 0
    grid = (ncols // tn, M // tm)
    cb0 = col0 // tn
    in_specs = [pl.BlockSpec((tm, x.shape[1]), lambda n, m: (m, 0)) for x in xs]
    in_specs += [pl.BlockSpec((None, w.shape[1], tn), lambda n, m: (0, 0, cb0 + n)) for w in ws]
    in_specs += [pl.BlockSpec(bs, im) for _, bs, im in extras]
    kernel = functools.partial(_ws_kernel, n_x=len(xs), x_index=tuple(x_index),
                               n_extra=len(extras), epilogue=epilogue)
    return pl.pallas_call(
        kernel,
        grid=grid,
        in_specs=in_specs,
        out_specs=pl.BlockSpec((tm, tn), lambda n, m: (m, n)),
        out_shape=jax.ShapeDtypeStruct((M, ncols), out_dtype),
        scratch_shapes=[pltpu.VMEM((w.shape[1], tn), BF16) for w in ws],
        compiler_params=_params(2),
        name=name,
    )(*xs, *ws, *[e for e, _, _ in extras])


def _ep_identity(prods, extras):
    return prods[0]


def _ep_swiglu(prods, extras):
    g, u = prods
    return g * jax.nn.sigmoid(g) * u


def _ep_sigmoid(prods, extras):
    return jax.nn.sigmoid(prods[0])


def _ep_merge(prods, extras):
    return prods[0] * extras[0].astype(F32) + prods[1] * extras[1].astype(F32)


def _ep_scale(scale, prods, extras):
    return prods[0] * scale


def _ep_rope(n_q_blocks, q_scale, prods, extras):
    cos, sin_signed = extras
    y = prods[0]
    half = DA_HEAD_DIM // 2
    outs = []
    for c in range(y.shape[1] // DA_HEAD_DIM):
        x = y[:, c * DA_HEAD_DIM:(c + 1) * DA_HEAD_DIM]
        outs.append(x * cos + pltpu.roll(x, half, 1) * sin_signed)
    y = jnp.concatenate(outs, axis=1) if len(outs) > 1 else outs[0]
    scale = jnp.where(pl.program_id(0) < n_q_blocks, q_scale, 1.0).astype(F32)
    return y * scale


def _res_ln_kernel(x_ref, w_ref, res_ref, g_ref, b_ref, o32_ref, ob_ref, *,
                   nk, res_scale, mm_scale):
    k = pl.program_id(1)
    n_total = o32_ref.shape[1]
    nc = min(n_total, RES_LN_COLS)
    for j in range(n_total // nc):
        cols = slice(j * nc, (j + 1) * nc)
        part = jnp.dot(x_ref[...], w_ref[:, cols].astype(BF16), preferred_element_type=F32)

        @pl.when(k == 0)
        def _():
            o32_ref[:, cols] = part

        @pl.when(k > 0)
        def _():
            o32_ref[:, cols] += part

    @pl.when(k == nk - 1)
    def _():
        gain = g_ref[...]
        bias = b_ref[...]

        def body(r, carry):
            rows = pl.ds(pl.multiple_of(r * LN_ROWS, LN_ROWS), LN_ROWS)
            y = res_scale * res_ref[rows, :] + mm_scale * o32_ref[rows, :]
            mu = jnp.mean(y, axis=-1, keepdims=True)
            yc = y - mu
            var = jnp.mean(yc * yc, axis=-1, keepdims=True)
            out = yc * lax.rsqrt(var + LN_EPS) * gain + bias
            o32_ref[rows, :] = out
            ob_ref[rows, :] = out.astype(BF16)
            return carry

        lax.fori_loop(0, o32_ref.shape[0] // LN_ROWS, body, 0, unroll=2)


def matmul_res_ln(name, x, w, res, g, b, tm, tk, res_scale, mm_scale):
    M, K = x.shape
    N = w.shape[2]
    assert M % tm == 0 and K % tk == 0
    nk = K // tk
    kernel = functools.partial(_res_ln_kernel, nk=nk, res_scale=res_scale, mm_scale=mm_scale)
    return pl.pallas_call(
        kernel,
        grid=(M // tm, nk),
        in_specs=[
            pl.BlockSpec((tm, tk), lambda m, k: (m, k)),
            pl.BlockSpec((None, tk, N), lambda m, k: (0, k, 0)),
            pl.BlockSpec((tm, N), lambda m, k: (m, 0), pipeline_mode=pl.Buffered(1)),
            pl.BlockSpec((1, N), lambda m, k: (0, 0)),
            pl.BlockSpec((1, N), lambda m, k: (0, 0)),
        ],
        out_specs=[
            pl.BlockSpec((tm, N), lambda m, k: (m, 0), pipeline_mode=pl.Buffered(1)),
            pl.BlockSpec((tm, N), lambda m, k: (m, 0), pipeline_mode=pl.Buffered(1)),
        ],
        out_shape=[jax.ShapeDtypeStruct((M, N), F32), jax.ShapeDtypeStruct((M, N), BF16)],
        compiler_params=_params(2),
        name=name,
    )(x, w, res, g, b)


def _dattn_kernel(q_ref, k_ref, v_ref, lq1_ref, lk1_ref, lq2_ref, lk2_ref, g_ref, o_ref,
                  m_ref, l_ref, acc_ref, *, tq, tk, lambda_init):
    qi = pl.program_id(1)
    ki = pl.program_id(2)
    d = DA_HEAD_DIM

    @pl.when(ki == 0)
    def _():
        m_ref[...] = jnp.full(m_ref.shape, -jnp.inf, F32)
        l_ref[...] = jnp.zeros(l_ref.shape, F32)
        acc_ref[...] = jnp.zeros(acc_ref.shape, F32)

    def update(masked):
        v = v_ref[...]
        if masked:
            row = lax.broadcasted_iota(jnp.int32, (tq, tk), 0)
            col = lax.broadcasted_iota(jnp.int32, (tq, tk), 1)
            keep = col <= row
        for j in range(2):
            s = _nt_dot(q_ref[:, j * d:(j + 1) * d], k_ref[:, j * d:(j + 1) * d])
            if masked:
                s = jnp.where(keep, s, -jnp.inf)
            m_prev = m_ref[j]
            m_new = jnp.maximum(m_prev, jnp.max(s, axis=-1, keepdims=True))
            alpha = jnp.exp(m_prev - m_new)
            p = jnp.exp(s - m_new)
            l_ref[j] = alpha * l_ref[j] + jnp.sum(p, axis=-1, keepdims=True)
            acc_ref[j] = alpha * acc_ref[j] + jnp.dot(p.astype(BF16), v, preferred_element_type=F32)
            m_ref[j] = m_new

    @pl.when(ki < qi)
    def _():
        update(False)

    @pl.when(ki == qi)
    def _():
        update(True)
        lam = (jnp.exp(jnp.sum(lq1_ref[...] * lk1_ref[...], axis=-1, keepdims=True))
               - jnp.exp(jnp.sum(lq2_ref[...] * lk2_ref[...], axis=-1, keepdims=True))
               + lambda_init)
        o = acc_ref[0] / l_ref[0] - lam * (acc_ref[1] / l_ref[1])
        ms = jnp.mean(o * o, axis=-1, keepdims=True)
        o = o * lax.rsqrt(ms + LN_EPS) * g_ref[...] * (1.0 - lambda_init)
        o_ref[...] = o.astype(o_ref.dtype)


def diff_attention(qk, v, lq1, lk1, lq2, lk2, subln_g, n_heads, tq, lambda_init):
    S = v.shape[0]
    dv = 2 * DA_HEAD_DIM
    nq = S // tq
    kernel = functools.partial(_dattn_kernel, tq=tq, tk=tq, lambda_init=lambda_init)
    vec = pl.BlockSpec((1, DA_HEAD_DIM), lambda h, qi, ki: (0, 0))
    return pl.pallas_call(
        kernel,
        grid=(n_heads, nq, nq),
        in_specs=[
            pl.BlockSpec((tq, dv), lambda h, qi, ki: (qi, h)),
            pl.BlockSpec((tq, dv), lambda h, qi, ki: (jnp.minimum(ki, qi), n_heads + h)),
            pl.BlockSpec((tq, dv), lambda h, qi, ki: (jnp.minimum(ki, qi), h)),
            vec, vec, vec, vec,
            pl.BlockSpec((1, dv), lambda h, qi, ki: (0, 0)),
        ],
        out_specs=pl.BlockSpec((tq, dv), lambda h, qi, ki: (qi, h)),
        out_shape=jax.ShapeDtypeStruct((S, n_heads * dv), BF16),
        scratch_shapes=[
            pltpu.VMEM((2, tq, 1), F32),
            pltpu.VMEM((2, tq, 1), F32),
            pltpu.VMEM((2, tq, dv), F32),
        ],
        compiler_params=_params(3),
        name="diff_attn",
    )(qk, qk, v, lq1, lk1, lq2, lk2, subln_g)


def _group_row_bcast(b, group, row):
    C, n = b.shape
    if group >= V7X_SUBLANES:
        b3 = b.reshape(C // group, group, n)
        return jnp.broadcast_to(b3[:, row:row + 1, :], (C // group, group, n)).reshape(C, n)
    b3 = b.reshape(C // V7X_SUBLANES, V7X_SUBLANES, n)
    sub = lax.broadcasted_iota(jnp.int32, b3.shape, 1)
    out = None
    for start in range(V7X_SUBLANES - group, -1, -group):
        r = jnp.broadcast_to(b3[:, start + row:start + row + 1, :], b3.shape)
        out = r if out is None else jnp.where(sub < start + group, r, out)
    return out.reshape(C, n)


def _hgrn_kernel(q_ref, f_ref, i_ref, g_ref, lb_ref, ng_ref, o_ref, st_ref, *, layer, n_chunks):
    C = HG_CHUNK

    @pl.when(pl.program_id(1) == 0)
    def _():
        st_ref[...] = jnp.zeros(st_ref.shape, F32)

    lg = lb_ref[...]
    e = jnp.exp(lg - jnp.max(lg, axis=0, keepdims=True))
    sm = e / jnp.sum(e, axis=0, keepdims=True)
    lb = jnp.sum(sm[:layer + 1], axis=0, keepdims=True)

    row = lax.broadcasted_iota(jnp.int32, (C, C), 0)
    col = lax.broadcasted_iota(jnp.int32, (C, C), 1)
    tri = (col <= row).astype(BF16)
    levels = []
    h = C // 2
    while h >= 1:
        levels.append(h)
        h //= 2
    masks = [((row // (2 * h)) == (col // (2 * h))) & ((row % (2 * h)) >= h) & ((col % (2 * h)) < h)
             for h in levels]
    diag = row == col
    norm_g = ng_ref[...]

    def chunk(c, carry):
        rows = pl.ds(pl.multiple_of(c * C, C), C)
        q = q_ref[rows, :]
        f = lb + (1.0 - lb) * jax.nn.sigmoid(f_ref[rows, :])
        k = 1.0 - f
        logf = jnp.log(f)
        v = i_ref[rows, :].astype(BF16)

        hi = logf.astype(BF16)
        r1 = logf - hi.astype(F32)
        mid = r1.astype(BF16)
        lo = (r1 - mid.astype(F32)).astype(BF16)
        b = (jnp.dot(tri, hi, preferred_element_type=F32)
             + jnp.dot(tri, mid, preferred_element_type=F32)
             + jnp.dot(tri, lo, preferred_element_type=F32))

        a = jnp.where(diag, _nt_dot(q.astype(BF16), k.astype(BF16)), 0.0)
        for h, mask in zip(levels, masks):
            bref = _group_row_bcast(b, 2 * h, h - 1)
            qh = q * jnp.exp(jnp.minimum(b - bref, 0.0))
            kh = k * jnp.exp(jnp.minimum(bref - b, 0.0))
            a = jnp.where(mask, _nt_dot(qh.astype(BF16), kh.astype(BF16)), a)

        st = st_ref[...]
        o = _nt_dot((q * jnp.exp(b)).astype(BF16), st.astype(BF16))
        o = o + jnp.dot(a.astype(BF16), v, preferred_element_type=F32)

        b_last = b[C - 1:C, :]
        kd = (k * jnp.exp(b_last - b)).astype(BF16)
        st_ref[...] = st * jnp.exp(b_last) + _tn_dot(v, kd)

        ms = jnp.mean(o * o, axis=-1, keepdims=True)
        gate = g_ref[rows, :]
        o = o * lax.rsqrt(ms + LN_EPS) * norm_g * (gate * jax.nn.sigmoid(gate))
        o_ref[rows, :] = o.astype(o_ref.dtype)
        return carry

    lax.fori_loop(0, n_chunks, chunk, 0)


def hgrn2(hg, lb_logits, norm_g, n_heads, layer, t_block):
    S = hg.shape[0]
    d = HG_DIM
    assert t_block % HG_CHUNK == 0 and S % t_block == 0
    kernel = functools.partial(_hgrn_kernel, layer=layer, n_chunks=t_block // HG_CHUNK)

    def part(p):
        return pl.BlockSpec((t_block, d), lambda h, t: (t, p * n_heads + h))

    return pl.pallas_call(
        kernel,
        grid=(n_heads, S // t_block),
        in_specs=[
            part(0), part(1), part(2), part(3),
            pl.BlockSpec((lb_logits.shape[0], d), lambda h, t: (0, h)),
            pl.BlockSpec((1, d), lambda h, t: (0, 0)),
        ],
        out_specs=pl.BlockSpec((t_block, d), lambda h, t: (t, h)),
        out_shape=jax.ShapeDtypeStruct((S, n_heads * d), BF16),
        scratch_shapes=[pltpu.VMEM((d, d), F32)],
        compiler_params=_params(2),
        name="hgrn2",
    )(hg, hg, hg, hg, lb_logits, norm_g)


def _xattn_kernel(q_ref, k_ref, v_ref, o_ref, *, n_heads):
    d = XA_HEAD_DIM
    outs = []
    for h in range(n_heads):
        s = _nt_dot(q_ref[:, h * d:(h + 1) * d], k_ref[:, h * d:(h + 1) * d])
        m = jnp.max(s, axis=-1, keepdims=True)
        p = jnp.exp(s - m)
        l = jnp.sum(p, axis=-1, keepdims=True)
        o = jnp.dot(p.astype(BF16), v_ref[:, h * d:(h + 1) * d], preferred_element_type=F32)
        outs.append(o / l)
    o_ref[...] = jnp.concatenate(outs, axis=1).astype(o_ref.dtype)


def cross_attention(q, k, v, n_heads, tm):
    S, W = q.shape
    Mm = k.shape[0]
    return pl.pallas_call(
        functools.partial(_xattn_kernel, n_heads=n_heads),
        grid=(S // tm,),
        in_specs=[
            pl.BlockSpec((tm, W), lambda m: (m, 0)),
            pl.BlockSpec((Mm, W), lambda m: (0, 0)),
            pl.BlockSpec((Mm, W), lambda m: (0, 0)),
        ],
        out_specs=pl.BlockSpec((tm, W), lambda m: (m, 0)),
        out_shape=jax.ShapeDtypeStruct((S, W), BF16),
        compiler_params=_params(1),
        name="xattn",
    )(q, k, v)


def _largest_divisor(n, candidates):
    for c in candidates:
        if n % c == 0:
            return c
    raise ValueError(f"no tile in {candidates} divides {n}")


def _rope_tables(seq, dim):
    inv = 1.0 / (ROPE_THETA ** (jnp.arange(0, dim, 2, dtype=F32) / dim))
    ang = jnp.arange(seq, dtype=F32)[:, None] * inv[None, :]
    cos, sin = jnp.cos(ang), jnp.sin(ang)
    return jnp.concatenate([cos, cos], axis=-1), jnp.concatenate([-sin, sin], axis=-1)


def kernel(x, mem, ffn1_w_gate, ffn1_w_up, ffn1_w_down, ln1_g, ln1_b, mix_w_in, da_lambda_q1, da_lambda_k1, da_lambda_q2, da_lambda_k2, da_subln_g, da_w_up, hg_lb_logits, hg_norm_g, hg_w_up, mix_w_out, ln2_g, ln2_b, xa_w_q, xa_w_k, xa_w_v, xa_w_o, ln3_g, ln3_b, ffn2_w_gate, ffn2_w_up, ffn2_w_down, ln4_g, ln4_b):
    B, S, D = x.shape
    depth = ffn1_w_gate.shape[0]
    assert B == 1 and depth == 1
    d_ff = ffn1_w_gate.shape[2]
    da_width = da_w_up.shape[1]
    hg_width = hg_w_up.shape[1]
    xa_width = xa_w_q.shape[2]
    da_heads = da_width // (2 * DA_HEAD_DIM)
    hg_heads = hg_width // HG_DIM
    xa_heads = xa_width // XA_HEAD_DIM
    assert mix_w_in.shape[2] == 3 * da_width + 4 * hg_width + 2 * D
    alpha = (2 * depth) ** 0.25
    layer = 0
    lambda_init = 0.8 - 0.6 * math.exp(-0.3 * layer)

    tm = _largest_divisor(S, (1024, 512, 256, 128))
    tm_ln = _largest_divisor(S, (512, 256, 128))
    tf = _largest_divisor(d_ff, (256, 128))
    tn = _largest_divisor(da_width, (512, 256, 128))
    tq = _largest_divisor(S, (512, 256, 128))
    t_hg = _largest_divisor(S, (1024, 512, 256, 128))

    def ffn(name, h32, hb, w_gate, w_up, w_down, g, b):
        act = ws_matmul(name + "_up", [hb], [w_gate, w_up], (0, 0), 0, d_ff, tm, tf, BF16, _ep_swiglu)
        return matmul_res_ln(name + "_down_ln", act, w_down, h32, g, b, tm_ln, tf, alpha, 0.5)

    x32 = x[0]
    xb = x32.astype(BF16)
    h1, h1b = ffn("ffn1", x32, xb, ffn1_w_gate, ffn1_w_up, ffn1_w_down, ln1_g, ln1_b)

    cos, sin_signed = _rope_tables(S, DA_HEAD_DIM)
    rope_ep = functools.partial(_ep_rope, da_width // tn, DA_HEAD_DIM ** -0.5)
    rope_spec = ((tm, DA_HEAD_DIM), lambda n, m: (m, 0))
    qk = ws_matmul("mix_in_qk", [h1b], [mix_w_in], (0,), 0, 2 * da_width, tm, tn, BF16, rope_ep,
                   extras=[(cos, *rope_spec), (sin_signed, *rope_spec)])
    c0 = 2 * da_width
    v = ws_matmul("mix_in_v", [h1b], [mix_w_in], (0,), c0, da_width, tm, tn, BF16, _ep_identity)
    c0 += da_width
    hg = ws_matmul("mix_in_hg", [h1b], [mix_w_in], (0,), c0, 4 * hg_width, tm, tn, F32, _ep_identity)
    c0 += 4 * hg_width
    gates = ws_matmul("mix_in_gates", [h1b], [mix_w_in], (0,), c0, 2 * D, tm, tn, BF16, _ep_sigmoid)

    o_a = diff_attention(qk, v, da_lambda_q1, da_lambda_k1, da_lambda_q2, da_lambda_k2,
                         da_subln_g, da_heads, tq, lambda_init)
    o_b = hgrn2(hg, hg_lb_logits, hg_norm_g, hg_heads, layer, t_hg)

    nb = D // tn
    merged = ws_matmul("merge", [o_a, o_b], [da_w_up, hg_w_up], (0, 1), 0, D, tm, tn, BF16, _ep_merge,
                       extras=[(gates, (tm, tn), lambda n, m: (m, n)),
                               (gates, (tm, tn), lambda n, m: (m, nb + n))])
    h2, h2b = matmul_res_ln("mix_out_ln", merged, mix_w_out, h1, ln2_g, ln2_b, tm_ln, 256, alpha, 1.0)

    memb = mem[0].astype(BF16)
    xq = ws_matmul("xa_q", [h2b], [xa_w_q], (0,), 0, xa_width, tm, xa_width, BF16,
                   functools.partial(_ep_scale, XA_HEAD_DIM ** -0.5))
    xk = ws_matmul("xa_k", [memb], [xa_w_k], (0,), 0, xa_width, memb.shape[0], xa_width, BF16, _ep_identity)
    xv = ws_matmul("xa_v", [memb], [xa_w_v], (0,), 0, xa_width, memb.shape[0], xa_width, BF16, _ep_identity)
    xo = cross_attention(xq, xk, xv, xa_heads, tm)
    h3, h3b = matmul_res_ln("xa_o_ln", xo, xa_w_o, h2, ln3_g, ln3_b, tm_ln, xa_width, alpha, 1.0)

    h4, _ = ffn("ffn2", h3, h3b, ffn2_w_gate, ffn2_w_up, ffn2_w_down, ln4_g, ln4_b)
    return h4[None]
```
